```python
import math
import jax, jax.numpy as jnp
from jax import lax
import numpy as np

D_MODEL = 1024
BATCH = 32
SEQ = 2048
DEPTH = 1

CTX_LEN = 256
GRID_W = 64
D_MIX = D_MODEL
RW_N = 64
RW_W = D_MIX // 2
RW_H = RW_W // RW_N
LORA_W = 64
LORA_A = 64
LORA_G = 128
TSHIFT = 3
M_INNER = D_MIX - RW_W
M_P = 64
M_H = M_INNER // M_P
M_G = 2
M_N = 128
CHUNK = 128
M_CONV = 3
M_CONV_CH = M_INNER + 2 * M_G * M_N
RW_SIZES = (RW_W, RW_W, RW_W, LORA_W, LORA_W, LORA_A, LORA_A, LORA_G)
M_SIZES = (M_INNER, M_CONV_CH, M_H, M_H)
RW_COLS = sum(RW_SIZES)
M_COLS = sum(M_SIZES)
IN_COLS = RW_COLS + M_COLS
D_FF = ((8 * D_MODEL + 3 * 256 - 1) // (3 * 256)) * 256
N_MOD = 6
NORM_EPS = 1e-6
RW_LN_EPS = 64e-5

kernel_name = 'rwkv7_mamba2_hybrid_dit_block'


def rms_normalize(x, eps=NORM_EPS):
    xf = x.astype(jnp.float32)
    return (xf * lax.rsqrt(jnp.mean(xf * xf, axis=-1, keepdims=True) + eps)).astype(x.dtype)


def rmsnorm(x, w):
    return rms_normalize(x) * w


def modulate(h, shift, scale):
    return h * (1.0 + scale) + shift


def split_cols(z, sizes):
    return jnp.split(z, np.cumsum(sizes)[:-1].tolist(), axis=-1)


def dwconv1d(x, w):
    k = w.shape[0]
    return lax.conv_general_dilated(x, w[:, None, :], window_strides=(1,), padding=[(k // 2, k // 2)],
                                    dimension_numbers=('NWC', 'WIO', 'NWC'), feature_group_count=x.shape[-1])


def dwconv2d(x, w, rows):
    b, t, ch = x.shape
    k = w.shape[0]
    y = lax.conv_general_dilated(x.reshape(b, rows, t // rows, ch), w[:, :, None, :], window_strides=(1, 1),
                                 padding=[(k // 2, k // 2), (k // 2, k // 2)],
                                 dimension_numbers=('NHWC', 'HWIO', 'NHWC'), feature_group_count=ch)
    return y.reshape(b, t, ch)


def swiglu(h, w_gu, w_down):
    gate, up = jnp.split(h @ w_gu, 2, axis=-1)
    return (jax.nn.silu(gate) * up) @ w_down


def rwkv_scan(r, w, k, v, kk, a, s0, reverse, emit):
    kka = kk * a

    def step(s, inp):
        w_t, k_t, v_t, kk_t, kka_t, *rest = inp
        sa = jnp.einsum('bhij,bhj->bhi', s, kk_t)
        s = s * w_t[:, :, None, :] - sa[..., None] * kka_t[:, :, None, :] + v_t[..., None] * k_t[:, :, None, :]
        y = jnp.einsum('bhij,bhj->bhi', s, rest[0]) if emit else None
        return s, y

    seqs = (w, k, v, kk, kka) + ((r,) if emit else ())
    s, y = lax.scan(step, s0, tuple(jnp.swapaxes(z, 0, 1) for z in seqs), reverse=reverse)
    return (jnp.swapaxes(y, 0, 1) if emit else None), s


def rwkv_branch(u, s0_f, s0_b, emit, tshift_w, w0, w2, a0, a2, g2, k_k, k_a, r_k, lnx_w, lnx_b):
    b, t, _ = u.shape
    u = dwconv1d(u, tshift_w)
    r, k, v, wd_f, wd_b, ad_f, ad_b, gd = split_cols(u, RW_SIZES)
    heads = lambda z: z.reshape(b, t, RW_H, RW_N)
    kkf = heads(k * k_k).astype(jnp.float32)
    kk = (kkf / jnp.maximum(jnp.sqrt(jnp.sum(kkf * kkf, -1, keepdims=True)), 1e-12)).astype(u.dtype)
    ys, finals = [], []
    for d, (wd, ad, s0) in enumerate(((wd_f, ad_f, s0_f), (wd_b, ad_b, s0_b))):
        log_w = -jax.nn.softplus(-(w0[d] + jnp.tanh(wd) @ w2[d])) - 0.5
        a = jax.nn.sigmoid(a0[d] + ad @ a2[d])
        k_d = k * (1.0 + (a - 1.0) * k_a)
        y, s = rwkv_scan(heads(r), heads(jnp.exp(-jnp.exp(log_w))), heads(k_d), heads(v), kk, heads(a),
                         s0, d == 1, emit)
        ys.append(y)
        finals.append(s)
    if not emit:
        return None, finals[0], finals[1]
    yf = (ys[0] + ys[1]).astype(jnp.float32)
    mu = jnp.mean(yf, -1, keepdims=True)
    var = jnp.mean(jnp.square(yf - mu), -1, keepdims=True)
    y = ((yf - mu) * lax.rsqrt(var + RW_LN_EPS)).astype(u.dtype).reshape(b, t, RW_W) * lnx_w + lnx_b
    bonus = jnp.sum(heads(r) * heads(k) * r_k, -1, keepdims=True) * heads(v)
    g = jax.nn.sigmoid(gd) @ g2
    return (y + bonus.reshape(b, t, RW_W)) * g, finals[0], finals[1]


def segsum_from_cumsum(cs):
    n = cs.shape[-1]
    diff = cs[..., :, None] - cs[..., None, :]
    return jnp.where(jnp.tril(jnp.ones((n, n), dtype=bool)), diff, -jnp.inf)


def ssd(xh, log_a, bmat, cmat, h0, with_output):
    b, t, nh, p = xh.shape
    g, n = bmat.shape[2], bmat.shape[3]
    hg = nh // g
    nc = t // CHUNK
    dtp = xh.dtype
    X = xh.reshape(b, nc, CHUNK, g, hg, p)
    Bc = bmat.reshape(b, nc, CHUNK, g, n)
    A = log_a.astype(jnp.float32).reshape(b, nc, CHUNK, g, hg).transpose(0, 3, 4, 1, 2)
    a_cs = jnp.cumsum(A, axis=-1)
    decay_states = jnp.exp(a_cs[..., -1:] - a_cs).astype(dtp)
    states = jnp.einsum('bclgn,bgjcl,bclgjp->bcgjpn', Bc, decay_states, X)
    states = jnp.concatenate([h0.reshape(b, 1, g, hg, p, n), states], axis=1)
    chunk_cs = jnp.cumsum(jnp.pad(a_cs[..., -1], ((0, 0), (0, 0), (0, 0), (1, 0))), axis=-1)
    decay_chunk = jnp.exp(segsum_from_cumsum(chunk_cs)).astype(dtp)
    new_states = jnp.einsum('bgjzc,bcgjpn->bzgjpn', decay_chunk, states)
    final = new_states[:, -1].reshape(b, nh, p, n)
    if not with_output:
        return None, final
    Cc = cmat.reshape(b, nc, CHUNK, g, n)
    lmat = jnp.exp(segsum_from_cumsum(a_cs)).astype(dtp)
    cb = jnp.einsum('bclgn,bcsgn->bgcls', Cc, Bc)
    y_diag = jnp.einsum('bgjcls,bcsgjp->bclgjp', cb[:, :, None] * lmat, X)
    y_off = jnp.einsum('bclgn,bcgjpn,bgjcl->bclgjp', Cc, new_states[:, :-1], jnp.exp(a_cs).astype(dtp))
    return (y_diag + y_off).reshape(b, t, nh, p), final


def mamba_branch(u, rows, h0_f, h0_b, emit, conv_w, conv_b, dt_bias, a_log, d_skip, gnorm_w):
    b, t, _ = u.shape
    z, xbc, dt_f, dt_b = split_cols(u, M_SIZES)
    xbc = dwconv1d(xbc, conv_w[M_CONV // 2]) if rows is None else dwconv2d(xbc, conv_w, rows)
    xs, bm, cm = split_cols(jax.nn.silu(xbc + conv_b), (M_INNER, M_G * M_N, M_G * M_N))
    xh = xs.reshape(b, t, M_H, M_P)
    bm = bm.reshape(b, t, M_G, M_N)
    cm = cm.reshape(b, t, M_G, M_N)
    ys, finals = [], []
    for d, (dt_raw, h0) in enumerate(((dt_f, h0_f), (dt_b, h0_b))):
        dt = jax.nn.softplus(dt_raw + dt_bias[d])
        seqs = (xh * dt[..., None], -dt * jnp.exp(a_log[d]), bm, cm)
        if d == 1:
            seqs = tuple(s[:, ::-1] for s in seqs)
        y, hf = ssd(*seqs, h0, emit)
        ys.append(y[:, ::-1] if (emit and d == 1) else y)
        finals.append(hf)
    if not emit:
        return None, finals[0], finals[1]
    y = (ys[0] + ys[1] + d_skip[:, None] * xh).reshape(b, t, M_INNER) * jax.nn.silu(z)
    y = rms_normalize(y.reshape(b, t, M_G, M_INNER // M_G)).reshape(b, t, M_INNER) * gnorm_w
    return y, finals[0], finals[1]


def setup_inputs(seed: int = 0) -> dict:
    key = jax.random.key(seed)
    ks = iter(jax.random.split(key, 40))

    def nrm(shape, scale):
        return jax.random.normal(next(ks), shape, jnp.float32) * scale

    def unif(shape, lo, hi):
        return jax.random.uniform(next(ks), shape, jnp.float32, lo, hi)

    L = DEPTH
    x = nrm((BATCH, SEQ, D_MODEL), 1.0)
    c = nrm((BATCH, D_MODEL), 1.0)
    ctx = nrm((BATCH, CTX_LEN, D_MODEL), 1.0)
    c_ctx = nrm((D_MODEL,), 1.0)
    mod_w = nrm((L, D_MODEL, N_MOD * D_MODEL), 0.5 * D_MODEL ** -0.5)
    mod_b = nrm((L, N_MOD * D_MODEL), 0.01)
    norm1_w = 1.0 + nrm((L, D_MODEL), 0.02)
    w_in = nrm((L, D_MODEL, IN_COLS), D_MODEL ** -0.5)
    mix = unif((L, 1, RW_COLS), 0.2, 0.8)
    tshift_w = jnp.concatenate([0.5 * mix, 1.0 - mix, 0.5 * mix], axis=1) + nrm((L, TSHIFT, RW_COLS), 0.02)
    w0 = unif((L, 2, RW_W), -5.0, 0.5)
    w2 = nrm((L, 2, LORA_W, RW_W), 0.1 * LORA_W ** -0.5)
    a0 = nrm((L, 2, RW_W), 0.1)
    a2 = nrm((L, 2, LORA_A, RW_W), 0.1 * LORA_A ** -0.5)
    g2 = nrm((L, LORA_G, RW_W), LORA_G ** -0.5)
    k_k = 0.85 + nrm((L, RW_W), 0.05)
    k_a = 1.0 + nrm((L, RW_W), 0.05)
    r_k = nrm((L, RW_H, RW_N), 0.1)
    lnx_w = 1.0 + nrm((L, RW_W), 0.02)
    lnx_b = nrm((L, RW_W), 0.01)
    conv_w = nrm((L, M_CONV, M_CONV, M_CONV_CH), 1.0 / M_CONV)
    conv_b = nrm((L, M_CONV_CH), 0.01)
    dt0 = jnp.exp(unif((L, 2, M_H), math.log(1e-3), math.log(1e-1)))
    dt_bias = dt0 + jnp.log(-jnp.expm1(-dt0))
    a_log = jnp.log(unif((L, 2, M_H), 1.0, 16.0))
    d_skip = 1.0 + nrm((L, M_H), 0.1)
    gnorm_w = 1.0 + nrm((L, M_INNER), 0.02)
    w_out = nrm((L, D_MIX, D_MODEL), D_MIX ** -0.5)
    norm2_w = 1.0 + nrm((L, D_MODEL), 0.02)
    w_gu = nrm((L, D_MODEL, 2 * D_FF), D_MODEL ** -0.5)
    w_down = nrm((L, D_FF, D_MODEL), D_FF ** -0.5)
    final_norm_w = 1.0 + nrm((D_MODEL,), 0.02)
    return {'x': x, 'c': c, 'ctx': ctx, 'c_ctx': c_ctx, 'mod_w': mod_w, 'mod_b': mod_b, 'norm1_w': norm1_w,
            'w_in': w_in, 'tshift_w': tshift_w, 'w0': w0, 'w2': w2, 'a0': a0, 'a2': a2, 'g2': g2,
            'k_k': k_k, 'k_a': k_a, 'r_k': r_k, 'lnx_w': lnx_w, 'lnx_b': lnx_b, 'conv_w': conv_w,
            'conv_b': conv_b, 'dt_bias': dt_bias, 'a_log': a_log, 'd_skip': d_skip, 'gnorm_w': gnorm_w,
            'w_out': w_out, 'norm2_w': norm2_w, 'w_gu': w_gu, 'w_down': w_down, 'final_norm_w': final_norm_w}


def reference(x, c, ctx, c_ctx, mod_w, mod_b, norm1_w, w_in, tshift_w, w0, w2, a0, a2, g2, k_k, k_a, r_k,
              lnx_w, lnx_b, conv_w, conv_b, dt_bias, a_log, d_skip, gnorm_w, w_out, norm2_w, w_gu, w_down,
              final_norm_w):
    b, t, _ = x.shape
    rows = t // GRID_W
    for i in range(DEPTH):
        emit_ctx = i + 1 < DEPTH
        lat_mod = jnp.split(jax.nn.silu(c) @ mod_w[i] + mod_b[i], N_MOD, axis=-1)
        sh1, sc1, gt1, sh2, sc2, gt2 = (m[:, None, :] for m in lat_mod)
        csh1, csc1, cgt1, csh2, csc2, cgt2 = jnp.split(jax.nn.silu(c_ctx) @ mod_w[i] + mod_b[i], N_MOD, axis=-1)
        u = modulate(rmsnorm(x, norm1_w[i]), sh1, sc1) @ w_in[i]
        uc = modulate(rmsnorm(ctx, norm1_w[i]), csh1, csc1) @ w_in[i]
        rw_p = (tshift_w[i], w0[i], w2[i], a0[i], a2[i], g2[i], k_k[i], k_a[i], r_k[i], lnx_w[i], lnx_b[i])
        m_p = (conv_w[i], conv_b[i], dt_bias[i], a_log[i], d_skip[i], gnorm_w[i])
        s0 = jnp.zeros((b, RW_H, RW_N, RW_N), x.dtype)
        h0 = jnp.zeros((b, M_H, M_P, M_N), x.dtype)
        yc_rw, rs_f, rs_b = rwkv_branch(uc[..., :RW_COLS], s0, s0, emit_ctx, *rw_p)
        yc_m, ms_f, ms_b = mamba_branch(uc[..., RW_COLS:], None, h0, h0, emit_ctx, *m_p)
        y_rw, _, _ = rwkv_branch(u[..., :RW_COLS], rs_f, rs_b, True, *rw_p)
        y_m, _, _ = mamba_branch(u[..., RW_COLS:], rows, ms_f, ms_b, True, *m_p)
        x = x + gt1 * (jnp.concatenate([y_rw, y_m], axis=-1) @ w_out[i])
        x = x + gt2 * swiglu(modulate(rmsnorm(x, norm2_w[i]), sh2, sc2), w_gu[i], w_down[i])
        if emit_ctx:
            ctx = ctx + cgt1 * (jnp.concatenate([yc_rw, yc_m], axis=-1) @ w_out[i])
            ctx = ctx + cgt2 * swiglu(modulate(rmsnorm(ctx, norm2_w[i]), csh2, csc2), w_gu[i], w_down[i])
    return rmsnorm(x, final_norm_w)
```

```python
import functools

import jax
import jax.numpy as jnp
from jax import lax
from jax.experimental import pallas as pl
from jax.experimental.pallas import tpu as pltpu

F32 = jnp.float32
BF16 = jnp.bfloat16

LANE = 128
RW_N = 64
RW_L = 64
M_P = 64
M_L = 128
GRID_W = 64
NORM_EPS = 1e-6
RW_LN_EPS = 64e-5
VMEM_LIMIT = 56 * 1024 * 1024
NEG_BIG = -1e30


def _dot(a, b):
    return jnp.dot(a.astype(BF16), b.astype(BF16), preferred_element_type=F32)


def _dot_nt(a, b):
    return lax.dot_general(a.astype(BF16), b.astype(BF16), (((1,), (1,)), ((), ())),
                           preferred_element_type=F32)


def _dot_tn(a, b):
    return lax.dot_general(a.astype(BF16), b.astype(BF16), (((0,), (0,)), ((), ())),
                           preferred_element_type=F32)


def _split3(x):
    h1 = x.astype(BF16)
    r1 = x - h1.astype(F32)
    h2 = r1.astype(BF16)
    h3 = (r1 - h2.astype(F32)).astype(BF16)
    return h1, h2, h3


def _dot_exact_rhs(a01, x):
    h1, h2, h3 = _split3(x)
    return (jnp.dot(a01, h1, preferred_element_type=F32) + jnp.dot(a01, h2, preferred_element_type=F32)
            + jnp.dot(a01, h3, preferred_element_type=F32))


def _dot_exact_lhs_nt(x, a01):
    h1, h2, h3 = _split3(x)
    dn = (((1,), (1,)), ((), ()))
    return (lax.dot_general(h1, a01, dn, preferred_element_type=F32)
            + lax.dot_general(h2, a01, dn, preferred_element_type=F32)
            + lax.dot_general(h3, a01, dn, preferred_element_type=F32))


def _dot_exact_lhs(x, a01):
    h1, h2, h3 = _split3(x)
    return (jnp.dot(h1, a01, preferred_element_type=F32) + jnp.dot(h2, a01, preferred_element_type=F32)
            + jnp.dot(h3, a01, preferred_element_type=F32))


def _softplus(z):
    return jnp.maximum(z, 0.0) + jnp.log(1.0 + jnp.exp(-jnp.abs(z)))


def _silu(z):
    return z * jax.nn.sigmoid(z)


def _iota2(shape, dim):
    return lax.broadcasted_iota(jnp.int32, shape, dim)


def _mod_kernel(c_ref, w_ref, b_ref, o_ref):
    o_ref[...] = _dot(_silu(c_ref[...]), w_ref[...]) + b_ref[...]


def _mod_call(cc, mod_w, mod_b):
    rows, d = cc.shape
    n = mod_w.shape[1]
    bn = 1024
    return pl.pallas_call(
        _mod_kernel,
        grid=(n // bn,),
        in_specs=[pl.BlockSpec((rows, d), lambda j: (0, 0)),
                  pl.BlockSpec((d, bn), lambda j: (0, j)),
                  pl.BlockSpec((1, bn), lambda j: (0, j))],
        out_specs=pl.BlockSpec((rows, bn), lambda j: (0, j)),
        out_shape=jax.ShapeDtypeStruct((rows, n), F32),
        compiler_params=pltpu.CompilerParams(dimension_semantics=("arbitrary",), vmem_limit_bytes=VMEM_LIMIT),
        name="mod",
    )(cc, mod_w, mod_b)


def _norm_mod(x, nw, sh, sc):
    xn = x * lax.rsqrt(jnp.mean(x * x, axis=-1, keepdims=True) + NORM_EPS)
    return (xn * nw) * (1.0 + sc) + sh


def _inproj_kernel(x_ref, xp_ref, xn_ref, sh_ref, sc_ref, nw_ref, wrw_ref, wm_ref, wdt_ref, tw_ref,
                   urw_ref, um_ref, dtt_ref):
    i = pl.program_id(1)
    n_i = pl.num_programs(1)
    tm = x_ref.shape[1]
    nw, sh, sc = nw_ref[...], sh_ref[0], sc_ref[0]
    xm = _norm_mod(x_ref[0], nw, sh, sc).astype(BF16)
    um_ref[0] = jnp.dot(xm, wm_ref[...], preferred_element_type=F32)
    dtt_ref[0] = lax.dot_general(wdt_ref[...], xm, (((1,), (1,)), ((), ())), preferred_element_type=F32)
    u = jnp.dot(xm, wrw_ref[...], preferred_element_type=F32)
    xpm = _norm_mod(xp_ref[0], nw, sh, sc).astype(BF16)
    xnm = _norm_mod(xn_ref[0], nw, sh, sc).astype(BF16)
    up = jnp.dot(xpm, wrw_ref[...], preferred_element_type=F32)[7:8, :]
    un = jnp.dot(xnm, wrw_ref[...], preferred_element_type=F32)[0:1, :]
    up = jnp.where(i > 0, up, 0.0)
    un = jnp.where(i < n_i - 1, un, 0.0)
    row = _iota2((tm, 1), 0)
    u_dn = jnp.where(row == 0, up, pltpu.roll(u, 1, 0))
    u_up = jnp.where(row == tm - 1, un, pltpu.roll(u, tm - 1, 0))
    tw = tw_ref[...]
    urw_ref[0] = tw[0:1, :] * u_dn + tw[1:2, :] * u + tw[2:3, :] * u_up


def _inproj_call(x, sh, sc, nw, w_rw, w_m, w_dtt, tw, tm):
    b, t, d = x.shape
    n_rw, n_m, n_dt = w_rw.shape[1], w_m.shape[1], w_dtt.shape[0]
    nt = t // tm
    hb = tm // 8
    const = lambda shape: pl.BlockSpec(shape, lambda bi, i: (0,) * len(shape), pipeline_mode=pl.Buffered(1))
    return pl.pallas_call(
        _inproj_kernel,
        grid=(b, nt),
        in_specs=[pl.BlockSpec((1, tm, d), lambda bi, i: (bi, i, 0)),
                  pl.BlockSpec((1, 8, d), lambda bi, i: (bi, jnp.maximum(i * hb - 1, 0), 0)),
                  pl.BlockSpec((1, 8, d), lambda bi, i: (bi, jnp.minimum((i + 1) * hb, t // 8 - 1), 0)),
                  pl.BlockSpec((1, 1, d), lambda bi, i: (bi, 0, 0)),
                  pl.BlockSpec((1, 1, d), lambda bi, i: (bi, 0, 0)),
                  const((1, d)), const((d, n_rw)), const((d, n_m)), const((n_dt, d)), const((3, n_rw))],
        out_specs=[pl.BlockSpec((1, tm, n_rw), lambda bi, i: (bi, i, 0)),
                   pl.BlockSpec((1, tm, n_m), lambda bi, i: (bi, i, 0)),
                   pl.BlockSpec((1, n_dt, tm), lambda bi, i: (bi, 0, i))],
        out_shape=[jax.ShapeDtypeStruct((b, t, n_rw), F32),
                   jax.ShapeDtypeStruct((b, t, n_m), F32),
                   jax.ShapeDtypeStruct((b, n_dt, t), F32)],
        compiler_params=pltpu.CompilerParams(dimension_semantics=("arbitrary", "arbitrary"),
                                             vmem_limit_bytes=VMEM_LIMIT),
        name="inproj",
    )(x, x, x, sh, sc, nw, w_rw, w_m, w_dtt, tw)


def _group_sum(x, gmat):
    hi = x.astype(BF16)
    lo = (x - hi.astype(F32)).astype(BF16)
    return jnp.dot(hi, gmat, preferred_element_type=F32) + jnp.dot(lo, gmat, preferred_element_type=F32)


def _rwkv_kernel(rc_ref, kc_ref, vc_ref, lc_ref, rl_ref, kl_ref, vl_ref, ll_ref,
                 w0_ref, w2_ref, a0_ref, a2_ref, g2_ref, kk_ref, ka_ref, rk_ref, lnw_ref, lnb_ref,
                 out_ref,
                 tinv_s, pr_s, dpk_s, drkq_s, kq_s, vps_s, gam_s, y_s):
    L = RW_L
    P2 = 2 * L
    t_ctx, t_lat = rc_ref.shape[1], rl_ref.shape[1]
    ns_c, ns_l = t_ctx // L, t_lat // L
    ns = ns_c + ns_l

    lane = _iota2((L, LANE), 1)
    h0 = lane < RW_N
    rr, cc = _iota2((P2, P2), 0), _iota2((P2, P2), 1)
    same = (rr >> 6) == (cc >> 6)
    tt, ss = rr & (L - 1), cc & (L - 1)
    m_strict = (same & (tt > ss), same & (tt < ss))
    m_incl = (same & (tt >= ss), same & (tt <= ss))
    eye = (rr == cc).astype(F32)
    r64, c64 = _iota2((L, L), 0), _iota2((L, L), 1)
    tri = ((r64 >= c64).astype(BF16), (r64 <= c64).astype(BF16))
    gmat = ((_iota2((LANE, LANE), 0) >> 6) == (_iota2((LANE, LANE), 1) >> 6)).astype(BF16)
    k_k, k_a = kk_ref[...], ka_ref[...]

    def stack_heads(z):
        return jnp.concatenate([jnp.where(h0, z, 0.0), jnp.where(h0, 0.0, z)], axis=0)

    def prep(r_ref, k_ref, v_ref, l_ref, row0, slot):
        rows = pl.ds(row0, L)
        r, k, v = r_ref[0, rows, :], k_ref[0, rows, :], v_ref[0, rows, :]
        lo = l_ref[0, rows, :]
        twd = jnp.tanh(lo[:, 0:LANE])
        ad = lo[:, LANE:2 * LANE]
        kkf = k * k_k
        nrm = jnp.sqrt(_group_sum(kkf * kkf, gmat))
        kk = kkf / jnp.maximum(nrm, 1e-12)
        vps_s[slot] = stack_heads(v).astype(BF16)
        for d in range(2):
            lwraw = w0_ref[d:d + 1, :] + _dot(twd, w2_ref[d])
            lw = -jnp.exp(-_softplus(-lwraw) - 0.5)
            a = jax.nn.sigmoid(a0_ref[d:d + 1, :] + _dot(ad, a2_ref[d]))
            kd = k * (1.0 + (a - 1.0) * k_a)
            kka = kk * a
            cum = _dot_exact_rhs(tri[d], lw)
            cend = cum[L - 1:L, :] if d == 0 else cum[0:1, :]
            e_inv = jnp.exp(-cum)
            e_end = jnp.exp(cend - cum)
            p = kk * jnp.exp(cum - lw)
            q = kka * e_inv
            kh = kd * e_inv
            rt = r * jnp.exp(cum)
            pr = jnp.concatenate([stack_heads(p), stack_heads(rt)], axis=0).astype(BF16)
            qq = jnp.concatenate([q, q], axis=0)
            kk2 = jnp.concatenate([kh, kh], axis=0)
            raw_q = _dot_nt(pr, qq)
            raw_k = _dot_nt(pr, kk2)
            dpq = jnp.where(m_strict[d], raw_q[:P2], 0.0)
            drq = jnp.where(m_incl[d], raw_q[P2:], 0.0)
            dpk = jnp.where(m_strict[d], raw_k[:P2], 0.0)
            drk = jnp.where(m_incl[d], raw_k[P2:], 0.0)
            xpow = dpq
            tinv = eye - dpq
            for _ in range(5):
                xpow = _dot(xpow, xpow)
                tinv = tinv + _dot(tinv, xpow)
            tinv_s[d, slot] = tinv.astype(BF16)
            pr_s[d, slot] = pr
            dpk_s[d, slot] = dpk.astype(BF16)
            drkq_s[d, slot] = jnp.concatenate([drk, -drq], axis=1).astype(BF16)
            kq_s[d, slot] = jnp.concatenate([kd * e_end, kka * e_end], axis=0).astype(BF16)
            gam_s[d, slot] = jnp.broadcast_to(jnp.exp(cend), (8, LANE))

    def prep_ctx(s, carry):
        prep(rc_ref, kc_ref, vc_ref, lc_ref, pl.multiple_of(s * L, L), s)
        return carry

    def prep_lat(s, carry):
        prep(rl_ref, kl_ref, vl_ref, ll_ref, pl.multiple_of(s * L, L), s + ns_c)
        return carry

    lax.fori_loop(0, ns_c, prep_ctx, 0)
    lax.fori_loop(0, ns_l, prep_lat, 0)

    y_s[...] = jnp.zeros(y_s.shape, F32)

    def chunk_step(d, slot, s_bd, emit):
        vps = vps_s[slot]
        prs = _dot_nt(pr_s[d, slot], s_bd)
        rhs = prs[:P2] + _dot(dpk_s[d, slot], vps)
        u_ps = _dot(tinv_s[d, slot], rhs)
        u_tl = u_ps[:L] + u_ps[L:]
        v_tl = vps[:L].astype(F32) + vps[L:].astype(F32)
        if emit:
            vu = jnp.concatenate([vps, u_ps.astype(BF16)], axis=0)
            y_ps = prs[P2:] + _dot(drkq_s[d, slot], vu)
            rows = pl.ds(pl.multiple_of((slot - ns_c) * L, L), L)
            y_s[rows, :] = y_s[rows, :] + (y_ps[:L] + y_ps[L:])
        ds = _dot_tn(jnp.concatenate([v_tl, -u_tl], axis=0), kq_s[d, slot])
        return s_bd * gam_s[d, slot][0:1, :] + jnp.where(same, ds, 0.0)

    def scan_ctx(kk_i, carry):
        sf, sb = carry
        return (chunk_step(0, kk_i, sf, False), chunk_step(1, ns_c - 1 - kk_i, sb, False))

    def scan_lat(kk_i, carry):
        sf, sb = carry
        return (chunk_step(0, ns_c + kk_i, sf, True), chunk_step(1, ns - 1 - kk_i, sb, True))

    z = jnp.zeros((P2, P2), F32)
    carry = lax.fori_loop(0, ns_c, scan_ctx, (z, z))
    lax.fori_loop(0, ns_l, scan_lat, carry)

    te = 256
    lnw, lnb, rk = lnw_ref[...], lnb_ref[...], rk_ref[...]
    inv_n = 1.0 / RW_N

    def epi(i, carry):
        rows = pl.ds(pl.multiple_of(i * te, te), te)
        y = y_s[rows, :]
        mu = _group_sum(y, gmat) * inv_n
        yc = y - mu
        var = _group_sum(yc * yc, gmat) * inv_n
        ln = yc * lax.rsqrt(var + RW_LN_EPS) * lnw + lnb
        r, k, v = rl_ref[0, rows, :], kl_ref[0, rows, :], vl_ref[0, rows, :]
        bonus = _group_sum(r * k * rk, gmat) * v
        g = _dot(jax.nn.sigmoid(ll_ref[0, rows, 2 * LANE:3 * LANE]), g2_ref[...])
        out_ref[0, rows, :] = (ln + bonus) * g
        return carry

    lax.fori_loop(0, t_lat // te, epi, 0)


def _rwkv_call(uc, ul, w0, w2p, a0, a2p, g2, k_k, k_a, r_k, lnw, lnb):
    b, t_ctx, _ = uc.shape
    t_lat = ul.shape[1]
    rw_w = g2.shape[1]
    n_pair = rw_w // LANE
    ns = (t_ctx + t_lat) // RW_L
    P2 = 2 * RW_L
    cb = rw_w // LANE
    lora_blk = (3 * rw_w) // (3 * LANE)

    def seq_specs(t):
        return [pl.BlockSpec((1, t, LANE), lambda bi, p: (bi, 0, p)),
                pl.BlockSpec((1, t, LANE), lambda bi, p: (bi, 0, cb + p)),
                pl.BlockSpec((1, t, LANE), lambda bi, p: (bi, 0, 2 * cb + p)),
                pl.BlockSpec((1, t, 3 * LANE), lambda bi, p: (bi, 0, lora_blk))]

    vec = pl.BlockSpec((1, LANE), lambda bi, p: (0, p))
    in_specs = (seq_specs(t_ctx) + seq_specs(t_lat) + [
        pl.BlockSpec((2, LANE), lambda bi, p: (0, p)),
        pl.BlockSpec((2, LANE, LANE), lambda bi, p: (0, 0, p)),
        pl.BlockSpec((2, LANE), lambda bi, p: (0, p)),
        pl.BlockSpec((2, LANE, LANE), lambda bi, p: (0, 0, p)),
        pl.BlockSpec((LANE, LANE), lambda bi, p: (0, p)),
        vec, vec, vec, vec, vec])
    scratch = [pltpu.VMEM((2, ns, P2, P2), BF16),
               pltpu.VMEM((2, ns, 2 * P2, LANE), BF16),
               pltpu.VMEM((2, ns, P2, P2), BF16),
               pltpu.VMEM((2, ns, P2, 2 * P2), BF16),
               pltpu.VMEM((2, ns, P2, LANE), BF16),
               pltpu.VMEM((ns, P2, LANE), BF16),
               pltpu.VMEM((2, ns, 8, LANE), F32),
               pltpu.VMEM((t_lat, LANE), F32)]
    return pl.pallas_call(
        _rwkv_kernel,
        grid=(b, n_pair),
        in_specs=in_specs,
        out_specs=pl.BlockSpec((1, t_lat, LANE), lambda bi, p: (bi, 0, p)),
        out_shape=jax.ShapeDtypeStruct((b, t_lat, rw_w), F32),
        scratch_shapes=scratch,
        compiler_params=pltpu.CompilerParams(dimension_semantics=("arbitrary", "arbitrary"),
                                             vmem_limit_bytes=VMEM_LIMIT),
        name="rwkv",
    )(uc, uc, uc, uc, ul, ul, ul, ul, w0, w2p, a0, a2p, g2, k_k, k_a, r_k, lnw, lnb)


def _mamba_kernel(xc_ref, bc_ref, dc_ref, dtc_ref, zl_ref, xl_ref, bl_ref, cl_ref, dl_ref, dtl_ref,
                  cwx_ref, cwb_ref, cwc_ref, cbx_ref, cbb_ref, cbc_ref, bias_ref, alog_ref, biasc_ref, alogc_ref,
                  dsk_ref, gnw_ref,
                  out_ref,
                  pad_s, act_s, dtcol_s, dtrow_s, y_s):
    L = M_L
    t_ctx, t_lat = xc_ref.shape[1], xl_ref.shape[1]
    ns_c, ns_l = t_ctx // L, t_lat // L
    ns = ns_c + ns_l
    PADR = 128
    ch_x = xl_ref.shape[2]
    n_pair = ch_x // LANE
    ch = ch_x + 2 * LANE

    def fill_pad(refs, t):
        pad_s[0:PADR, :] = jnp.zeros((PADR, ch), F32)
        pad_s[PADR + t:PADR + t + PADR, :] = jnp.zeros((PADR, ch), F32)
        col = 0
        for ref in refs:
            w = ref.shape[2]
            pad_s[PADR:PADR + t, col:col + w] = ref[0]
            col += w

    HALO = 72
    WIN = L + 2 * HALO

    def conv_tile(i, grid_taps, out_row0):
        start = pl.multiple_of(PADR - HALO + i * L, 8)
        pos = _iota2((L, 1), 0) & (GRID_W - 1)
        not_first, not_last = pos != 0, pos != GRID_W - 1
        for c0 in range(0, ch, LANE):
            if c0 < ch_x:
                wref, bref, wc = cwx_ref, cbx_ref, c0
            elif c0 < ch_x + LANE:
                wref, bref, wc = cwb_ref, cbb_ref, 0
            else:
                wref, bref, wc = cwc_ref, cbc_ref, 0
            win = pad_s[pl.ds(start, WIN), c0:c0 + LANE]
            w_dn = pltpu.roll(win, 1, 0)
            w_up = pltpu.roll(win, WIN - 1, 0)
            acc = jnp.zeros((L, LANE), F32)
            for dr in ((0, 1, 2) if grid_taps else (1,)):
                o = HALO + (dr - 1) * GRID_W
                wt = lambda dc: wref[3 * dr + dc:3 * dr + dc + 1, wc:wc + LANE]
                left, mid, right = w_dn[o:o + L], win[o:o + L], w_up[o:o + L]
                if grid_taps:
                    left = jnp.where(not_first, left, 0.0)
                    right = jnp.where(not_last, right, 0.0)
                acc = acc + wt(0) * left + wt(1) * mid + wt(2) * right
            rows = pl.ds(pl.multiple_of(out_row0 + i * L, L), L)
            act_s[rows, c0:c0 + LANE] = _silu(acc + bref[0:1, wc:wc + LANE])

    fill_pad((xc_ref, bc_ref), t_ctx)
    pad_s[PADR:PADR + t_ctx, ch_x + LANE:ch] = jnp.zeros((t_ctx, LANE), F32)

    def conv_ctx(i, carry):
        conv_tile(i, False, 0)
        return carry

    lax.fori_loop(0, ns_c, conv_ctx, 0)
    fill_pad((xl_ref, bl_ref, cl_ref), t_lat)

    def conv_lat(i, carry):
        conv_tile(i, True, t_ctx)
        return carry

    lax.fori_loop(0, ns_l, conv_lat, 0)

    bias_r, bias_c = bias_ref[0, 0:1, :], biasc_ref[0, :, 0:1]
    dtcol_s[0:t_ctx, :] = _softplus(dc_ref[0] + bias_r)
    dtcol_s[t_ctx:t_ctx + t_lat, :] = _softplus(dl_ref[0] + bias_r)
    dtrow_s[:, 0:t_ctx] = _softplus(dtc_ref[0] + bias_c)
    dtrow_s[:, t_ctx:t_ctx + t_lat] = _softplus(dtl_ref[0] + bias_c)
    neg_a_r = -jnp.exp(alog_ref[0, 0:1, :])
    neg_a_c = -jnp.exp(alogc_ref[0, :, 0:1])

    rr, cc = _iota2((L, L), 0), _iota2((L, L), 1)
    tril = (rr >= cc).astype(BF16)
    triu = (rr <= cc).astype(BF16)
    m_dir = (rr >= cc, rr <= cc)
    lane = _iota2((L, LANE), 1)
    h0 = lane < M_P
    nh = 2 * n_pair

    y_s[...] = jnp.zeros(y_s.shape, F32)

    def ssd_step(d, slot, hs, emit):
        rows = pl.ds(pl.multiple_of(slot * L, L), L)
        bm = act_s[rows, ch_x:ch_x + LANE]
        cm = act_s[rows, ch_x + LANE:ch]
        dtc = dtcol_s[rows, :]
        tri = tril if d == 0 else triu
        cs_c = _dot_exact_rhs(tri, dtc * neg_a_r)
        cs_r = _dot_exact_lhs_nt(dtrow_s[:, rows] * neg_a_c, tri)
        cb = _dot_nt(cm, bm) if emit else None
        bm16 = bm.astype(BF16)
        e_row = L - 1 if d == 0 else 0
        new = []
        for p in range(n_pair):
            j0, j1 = d * nh + 2 * p, d * nh + 2 * p + 1
            pick = lambda a: jnp.where(h0, a[:, j0:j0 + 1], a[:, j1:j1 + 1])
            cs_l = pick(cs_c)
            cend = cs_l[e_row:e_row + 1, :]
            xdt = act_s[rows, p * LANE:(p + 1) * LANE] * pick(dtc)
            h_old = hs[p]
            if emit:
                ydiag = []
                for j in (j0, j1):
                    diff = cs_c[:, j:j + 1] - cs_r[j:j + 1, :]
                    g = cb * jnp.exp(jnp.where(m_dir[d], diff, NEG_BIG))
                    ydiag.append(_dot(g, xdt))
                y = jnp.where(h0, ydiag[0], ydiag[1]) + _dot(cm, h_old) * jnp.exp(cs_l)
                yrows = pl.ds(pl.multiple_of((slot - ns_c) * L, L), L)
                y_s[yrows, p * LANE:(p + 1) * LANE] = y_s[yrows, p * LANE:(p + 1) * LANE] + y
            new.append(h_old * jnp.exp(cend) + _dot_tn(bm16, xdt * jnp.exp(cend - cs_l)))
        return tuple(new)

    def scan_ctx(k, hs):
        return ssd_step(0, k, hs[:n_pair], False) + ssd_step(1, ns_c - 1 - k, hs[n_pair:], False)

    def scan_lat(k, hs):
        return ssd_step(0, ns_c + k, hs[:n_pair], True) + ssd_step(1, ns - 1 - k, hs[n_pair:], True)

    hs0 = tuple(jnp.zeros((LANE, LANE), F32) for _ in range(2 * n_pair))
    hs1 = lax.fori_loop(0, ns_c, scan_ctx, hs0)
    lax.fori_loop(0, ns_l, scan_lat, hs1)

    te = 256
    dsk, gnw = dsk_ref[...], gnw_ref[...]

    def epi(i, carry):
        rows = pl.ds(pl.multiple_of(i * te, te), te)
        arows = pl.ds(pl.multiple_of(t_ctx + i * te, L), te)
        y = (y_s[rows, :] + dsk * act_s[arows, 0:ch_x]) * _silu(zl_ref[0, rows, :])
        out_ref[0, rows, :] = y * lax.rsqrt(jnp.mean(y * y, axis=-1, keepdims=True) + NORM_EPS) * gnw
        return carry

    lax.fori_loop(0, t_lat // te, epi, 0)


def _mamba_call(umc, uml, dttc, dttl, cw9, cb, bias_r, alog_r, bias_c, alog_c, dsk, gnw, m_inner, n_state, n_grp):
    b, t_ctx, _ = umc.shape
    t_lat = uml.shape[1]
    ch_x = m_inner // n_grp
    xb = ch_x // LANE
    nhg = ch_x // M_P
    z_blk = lambda bi, g: (bi, 0, g)
    x_blk = lambda bi, g: (bi, 0, n_grp + g)
    b_blk = lambda bi, g: (bi, 0, 2 * n_grp * xb + g)
    c_blk = lambda bi, g: (bi, 0, 2 * n_grp * xb + n_grp + g)
    d_blk = lambda bi, g: (bi, 0, 2 * n_grp * xb + 2 * n_grp + g)
    assert n_state == LANE
    in_specs = [
        pl.BlockSpec((1, t_ctx, ch_x), x_blk), pl.BlockSpec((1, t_ctx, LANE), b_blk),
        pl.BlockSpec((1, t_ctx, LANE), d_blk), pl.BlockSpec((1, 2 * nhg, t_ctx), lambda bi, g: (bi, g, 0)),
        pl.BlockSpec((1, t_lat, ch_x), z_blk), pl.BlockSpec((1, t_lat, ch_x), x_blk),
        pl.BlockSpec((1, t_lat, LANE), b_blk), pl.BlockSpec((1, t_lat, LANE), c_blk),
        pl.BlockSpec((1, t_lat, LANE), d_blk), pl.BlockSpec((1, 2 * nhg, t_lat), lambda bi, g: (bi, g, 0)),
        pl.BlockSpec((9, ch_x), lambda bi, g: (0, g)),
        pl.BlockSpec((9, LANE), lambda bi, g: (0, n_grp * xb + g)),
        pl.BlockSpec((9, LANE), lambda bi, g: (0, n_grp * xb + n_grp + g)),
        pl.BlockSpec((1, ch_x), lambda bi, g: (0, g)),
        pl.BlockSpec((1, LANE), lambda bi, g: (0, n_grp * xb + g)),
        pl.BlockSpec((1, LANE), lambda bi, g: (0, n_grp * xb + n_grp + g)),
        pl.BlockSpec((1, 8, LANE), lambda bi, g: (g, 0, 0)),
        pl.BlockSpec((1, 8, LANE), lambda bi, g: (g, 0, 0)),
        pl.BlockSpec((1, 2 * nhg, LANE), lambda bi, g: (g, 0, 0)),
        pl.BlockSpec((1, 2 * nhg, LANE), lambda bi, g: (g, 0, 0)),
        pl.BlockSpec((1, ch_x), lambda bi, g: (0, g)),
        pl.BlockSpec((1, ch_x), lambda bi, g: (0, g)),
    ]
    ch = ch_x + 2 * LANE
    scratch = [pltpu.VMEM((t_lat + 256, ch), F32),
               pltpu.VMEM((t_ctx + t_lat, ch), F32),
               pltpu.VMEM((t_ctx + t_lat, LANE), F32),
               pltpu.VMEM((2 * nhg, t_ctx + t_lat), F32),
               pltpu.VMEM((t_lat, ch_x), F32)]
    return pl.pallas_call(
        _mamba_kernel,
        grid=(b, n_grp),
        in_specs=in_specs,
        out_specs=pl.BlockSpec((1, t_lat, ch_x), lambda bi, g: (bi, 0, g)),
        out_shape=jax.ShapeDtypeStruct((b, t_lat, m_inner), F32),
        scratch_shapes=scratch,
        compiler_params=pltpu.CompilerParams(dimension_semantics=("arbitrary", "arbitrary"),
                                             vmem_limit_bytes=VMEM_LIMIT),
        name="mamba",
    )(umc, umc, umc, dttc, uml, uml, uml, uml, uml, dttl,
      cw9, cw9, cw9, cb, cb, cb, bias_r, alog_r, bias_c, alog_c, dsk, gnw)


def _ffn_kernel(x_ref, yrw_ref, ym_ref, gt1_ref, sh2_ref, sc2_ref, gt2_ref, n2w_ref, fnw_ref,
                wo_ref, wgu_ref, wd_ref, o_ref):
    rw_w = yrw_ref.shape[2]
    d_ff = wd_ref.shape[0]
    x = x_ref[0]
    mix = _dot(yrw_ref[0], wo_ref[0:rw_w, :]) + _dot(ym_ref[0], wo_ref[rw_w:, :])
    x1 = x + gt1_ref[0] * mix
    xm = _norm_mod(x1, n2w_ref[...], sh2_ref[0], sc2_ref[0]).astype(BF16)
    fc = 256
    acc = jnp.zeros(x.shape, F32)
    for j in range(d_ff // fc):
        gate = jnp.dot(xm, wgu_ref[:, j * fc:(j + 1) * fc], preferred_element_type=F32)
        up = jnp.dot(xm, wgu_ref[:, d_ff + j * fc:d_ff + (j + 1) * fc], preferred_element_type=F32)
        acc = acc + _dot(_silu(gate) * up, wd_ref[j * fc:(j + 1) * fc, :])
    x2 = x1 + gt2_ref[0] * acc
    o_ref[0] = x2 * lax.rsqrt(jnp.mean(x2 * x2, axis=-1, keepdims=True) + NORM_EPS) * fnw_ref[...]


def _ffn_call(x, y_rw, y_m, gt1, sh2, sc2, gt2, n2w, fnw, w_out, w_gu, w_down, tm):
    b, t, d = x.shape
    rw_w, m_w = y_rw.shape[2], y_m.shape[2]
    tile = lambda w: pl.BlockSpec((1, tm, w), lambda bi, i: (bi, i, 0))
    per_b = pl.BlockSpec((1, 1, d), lambda bi, i: (bi, 0, 0))
    const = lambda shape: pl.BlockSpec(shape, lambda bi, i: (0,) * len(shape), pipeline_mode=pl.Buffered(1))
    return pl.pallas_call(
        _ffn_kernel,
        grid=(b, t // tm),
        in_specs=[tile(d), tile(rw_w), tile(m_w), per_b, per_b, per_b, per_b, const((1, d)), const((1, d)),
                  const(w_out.shape), const(w_gu.shape), const(w_down.shape)],
        out_specs=tile(d),
        out_shape=jax.ShapeDtypeStruct((b, t, d), F32),
        compiler_params=pltpu.CompilerParams(dimension_semantics=("arbitrary", "arbitrary"),
                                             vmem_limit_bytes=VMEM_LIMIT),
        name="ffn",
    )(x, y_rw, y_m, gt1, sh2, sc2, gt2, n2w, fnw, w_out, w_gu, w_down)


def _token_tile(t):
    for tm in (512, 256, 128):
        if t % tm == 0:
            return tm
    raise ValueError(f"sequence length {t} must be a multiple of 128")


def _mixers(x, c, ctx, c_ctx, mod_w, mod_b, norm1_w, w_in, tshift_w, w0, w2, a0, a2, g2, k_k, k_a, r_k, lnx_w, lnx_b,
            conv_w, conv_b, dt_bias, a_log, d_skip, gnorm_w):
    b, t, d = x.shape
    t_ctx = ctx.shape[1]
    assert mod_w.shape[0] == 1, "single-layer block"
    rw_w = g2.shape[2]
    lora_w, lora_a, lora_g = w2.shape[2], a2.shape[2], g2.shape[1]
    n_heads_m = dt_bias.shape[2]
    m_inner = gnorm_w.shape[1]
    conv_ch = conv_b.shape[1]
    n_grp = 2
    n_state = (conv_ch - m_inner) // (2 * n_grp)
    rw_cols = 3 * rw_w + 2 * lora_w + 2 * lora_a + lora_g
    assert lora_w == 64 and lora_a == 64 and lora_g == LANE and rw_w % LANE == 0
    assert t % M_L == 0 and t_ctx % M_L == 0 and t % GRID_W == 0
    nhg = n_heads_m // n_grp

    cc = jnp.concatenate([c, c_ctx[None, :]], axis=0)
    mod = _mod_call(cc, mod_w[0].astype(BF16), mod_b[0][None, :])
    lat = [m[:, None, :] for m in jnp.split(mod[:b], 6, axis=-1)]
    cmod = [jnp.broadcast_to(m[:, None, :], (b, 1, d)) for m in jnp.split(mod[b:], 6, axis=-1)]
    sh1, sc1, gt1, sh2, sc2, gt2 = lat

    wi = w_in[0]
    m0 = rw_cols
    dt0 = m0 + m_inner + conv_ch
    w_rw = wi[:, :rw_cols].astype(BF16)
    dt_cols = []
    for g in range(n_grp):
        sel = jnp.concatenate([wi[:, dt0 + g * nhg:dt0 + (g + 1) * nhg],
                               wi[:, dt0 + n_heads_m + g * nhg:dt0 + n_heads_m + (g + 1) * nhg]], axis=1)
        dt_cols.append(sel)
    w_m = jnp.concatenate([wi[:, m0:dt0]] + [jnp.pad(s, ((0, 0), (0, LANE - 2 * nhg))) for s in dt_cols],
                          axis=1).astype(BF16)
    w_dtt = jnp.concatenate(dt_cols, axis=1).T.astype(BF16)
    nw1 = norm1_w[0][None, :]
    tw = tshift_w[0]

    urw_l, um_l, dtt_l = _inproj_call(x, sh1, sc1, nw1, w_rw, w_m, w_dtt, tw, _token_tile(t))
    urw_c, um_c, dtt_c = _inproj_call(ctx, cmod[0], cmod[1], nw1, w_rw, w_m, w_dtt, tw, _token_tile(t_ctx))

    zpad = jnp.zeros((lora_w, rw_w), F32)
    w2p = jnp.stack([jnp.concatenate([w2[0, 0], zpad], 0), jnp.concatenate([zpad, w2[0, 1]], 0)]).astype(BF16)
    a2p = jnp.stack([jnp.concatenate([a2[0, 0], zpad], 0), jnp.concatenate([zpad, a2[0, 1]], 0)]).astype(BF16)
    row = lambda v: v.reshape(1, -1)
    y_rw = _rwkv_call(urw_c, urw_l, w0[0], w2p, a0[0], a2p, g2[0].astype(BF16), row(k_k[0]), row(k_a[0]),
                      row(r_k[0]), row(lnx_w[0]), row(lnx_b[0]))

    def per_group(v):
        return jnp.stack([jnp.concatenate([v[0, g * nhg:(g + 1) * nhg], v[1, g * nhg:(g + 1) * nhg]])
                          for g in range(n_grp)])
    bias_g, alog_g = per_group(dt_bias[0]), per_group(a_log[0])
    lane_layout = lambda v: jnp.broadcast_to(jnp.pad(v, ((0, 0), (0, LANE - 2 * nhg)))[:, None, :], (n_grp, 8, LANE))
    row_layout = lambda v: jnp.broadcast_to(v[:, :, None], (n_grp, 2 * nhg, LANE))
    y_m = _mamba_call(um_c, um_l, dtt_c, dtt_l, conv_w[0].reshape(9, conv_ch), conv_b[0][None, :],
                      lane_layout(bias_g), lane_layout(alog_g), row_layout(bias_g), row_layout(alog_g),
                      jnp.repeat(d_skip[0], M_P)[None, :], gnorm_w[0][None, :], m_inner, n_state, n_grp)
    return (urw_l, um_l, y_rw, y_m), (gt1, sh2, sc2, gt2)


def _stages(*args):
    return _mixers(*args)[0]


def kernel(x, c, ctx, c_ctx, mod_w, mod_b, norm1_w, w_in, tshift_w, w0, w2, a0, a2, g2, k_k, k_a, r_k, lnx_w, lnx_b, conv_w, conv_b, dt_bias, a_log, d_skip, gnorm_w, w_out, norm2_w, w_gu, w_down, final_norm_w):
    (_, _, y_rw, y_m), (gt1, sh2, sc2, gt2) = _mixers(
        x, c, ctx, c_ctx, mod_w, mod_b, norm1_w, w_in, tshift_w, w0, w2, a0, a2, g2, k_k, k_a, r_k, lnx_w, lnx_b,
        conv_w, conv_b, dt_bias, a_log, d_skip, gnorm_w)
    t = x.shape[1]
    return _ffn_call(x, y_rw, y_m, gt1, sh2, sc2, gt2, norm2_w[0][None, :], final_norm_w[None, :],
                     w_out[0].astype(BF16), w_gu[0].astype(BF16), w_down[0].astype(BF16), _token_tile(t))
```
